```python
import math
import jax, jax.numpy as jnp
from jax import lax
import numpy as np

D_MODEL = 2048
BATCH = 4
SEQ = 4096
DEPTH = 1

GRID_W = 64
CTX_LEN = 256
HEAD_DIM = 64
A_HEADS = 8
A_V_DIM = 2 * HEAD_DIM
B_HEADS = 16
B_KV_HEADS = 2
B_GROUP = B_HEADS // B_KV_HEADS
WINDOW = 128
Q_BLOCK = 128
ROPE_THETA = 10000.0
N_EXPERTS = 32
TOP_K = 4
D_FF = D_MODEL
SWIGLU_ALPHA = 1.702
SWIGLU_LIMIT = 7.0
MOE_BLOCK = 128
LN_EPS = 1e-5
NEG_INF = -1e30
DEEPNORM_ALPHA = (2 * DEPTH) ** 0.25
DEEPNORM_BETA = (8 * DEPTH) ** -0.25
A_Q = 2 * A_HEADS * HEAD_DIM
A_K = 2 * A_HEADS * HEAD_DIM
A_V = A_HEADS * A_V_DIM
B_Q = B_HEADS * HEAD_DIM
B_K = B_KV_HEADS * HEAD_DIM
B_V = B_KV_HEADS * HEAD_DIM
PROJ_SPLITS = (A_Q, A_K, A_V, B_Q, B_K, B_V, D_MODEL, D_MODEL)
PROJ_COLS = A_Q + A_K + A_V + B_Q + B_K + B_V + 2 * D_MODEL

kernel_name = "hybrid_diffattn_swa_sink_moe_dit_layer"

F32 = jnp.float32


def layer_norm(x, g, b):
    xf = x.astype(F32)
    mu = xf.mean(-1, keepdims=True)
    var = jnp.square(xf - mu).mean(-1, keepdims=True)
    return ((xf - mu) * lax.rsqrt(var + LN_EPS) * g + b).astype(x.dtype)


def rms_norm(x, w):
    xf = x.astype(F32)
    return (xf * lax.rsqrt(jnp.mean(xf * xf, -1, keepdims=True) + LN_EPS) * w).astype(x.dtype)


def axial_rope_tables(n_tok):
    rows = n_tok // GRID_W
    row = jnp.repeat(jnp.arange(rows), GRID_W).astype(F32)
    col = jnp.tile(jnp.arange(GRID_W), rows).astype(F32)
    half = HEAD_DIM // 2
    inv = ROPE_THETA ** (-jnp.arange(0, half, 2, dtype=F32) / half)
    ang_r = row[:, None] * inv
    ang_c = col[:, None] * inv
    return jnp.cos(ang_r), jnp.sin(ang_r), jnp.cos(ang_c), jnp.sin(ang_c)


def rotate(x, cos, sin):
    x1, x2 = jnp.split(x, 2, axis=-1)
    cos = cos[None, :, None, :]
    sin = sin[None, :, None, :]
    return jnp.concatenate([x1 * cos - x2 * sin, x2 * cos + x1 * sin], -1).astype(x.dtype)


def axial_rope(x, tables):
    cr, sr, cc, sc = tables
    xr, xc = jnp.split(x, 2, axis=-1)
    return jnp.concatenate([rotate(xr, cr, sr), rotate(xc, cc, sc)], -1)


def split_projection(p, n_tok):
    b = p.shape[0]
    offs, acc = [], 0
    for s in PROJ_SPLITS[:-1]:
        acc += s
        offs.append(acc)
    qa, ka, va, qb, kb, vb, ga, gb = jnp.split(p, offs, axis=-1)
    return (qa.reshape(b, n_tok, 2 * A_HEADS, HEAD_DIM),
            ka.reshape(b, n_tok, 2 * A_HEADS, HEAD_DIM),
            va.reshape(b, n_tok, A_HEADS, A_V_DIM),
            qb.reshape(b, n_tok, B_HEADS, HEAD_DIM),
            kb.reshape(b, n_tok, B_KV_HEADS, HEAD_DIM),
            vb.reshape(b, n_tok, B_KV_HEADS, HEAD_DIM),
            ga, gb)


def diff_combine(s, lam, v):
    p = jax.nn.softmax(s.astype(F32), axis=-1)
    b, m, q, k = p.shape
    p = p.reshape(b, m // 2, 2, q, k)
    a = p[:, :, 0] - lam * p[:, :, 1]
    return jnp.einsum('bhqk,bkhe->bqhe', a.astype(v.dtype), v)


def diff_output(o, subln_w, lam_init):
    b, n = o.shape[:2]
    return (rms_norm(o, subln_w) * (1.0 - lam_init)).reshape(b, n, A_V)


def diff_attention_latent(q, k, v, kc, vc, lam, subln_w, lam_init):
    b, n = q.shape[:2]
    nb = n // Q_BLOCK
    scale = HEAD_DIM ** -0.5
    k_all = jnp.concatenate([k, kc], axis=1)
    v_all = jnp.concatenate([v, vc], axis=1)
    q_blocks = q.reshape(b, nb, Q_BLOCK, 2 * A_HEADS, HEAD_DIM).swapaxes(0, 1)

    def block(qb):
        s = jnp.einsum('bqmd,bkmd->bmqk', qb, k_all) * scale
        return diff_combine(s, lam, v_all)

    o = lax.map(block, q_blocks)
    o = o.swapaxes(0, 1).reshape(b, n, A_HEADS, A_V_DIM)
    return diff_output(o, subln_w, lam_init)


def diff_attention_ctx(qc, kc, vc, lam, subln_w, lam_init):
    s = jnp.einsum('bqmd,bkmd->bmqk', qc, kc) * (HEAD_DIM ** -0.5)
    return diff_output(diff_combine(s, lam, vc), subln_w, lam_init)


def sink_softmax(s, sinks):
    sink = jnp.broadcast_to(sinks.astype(F32).reshape(B_KV_HEADS, B_GROUP, 1, 1), s.shape[:-1] + (1,))
    p = jax.nn.softmax(jnp.concatenate([s, sink], axis=-1), axis=-1)
    return p[..., :-1]


def window_attention_latent(q, k, v, kc, vc, sinks):
    b, n = q.shape[:2]
    nb = n // WINDOW
    scale = HEAD_DIM ** -0.5
    qb = q.reshape(b, nb, WINDOW, B_KV_HEADS, B_GROUP, HEAD_DIM)

    def bands(t):
        tp = jnp.pad(t, ((0, 0), (WINDOW, WINDOW), (0, 0), (0, 0)))
        tp = tp.reshape(b, nb + 2, WINDOW, B_KV_HEADS, HEAD_DIM)
        return jnp.concatenate([tp[:, :-2], tp[:, 1:-1], tp[:, 2:]], axis=2)

    kw, vw = bands(k), bands(v)
    s_win = jnp.einsum('bnqhgd,bnjhd->bnhgqj', qb, kw).astype(F32) * scale
    s_ctx = jnp.einsum('bnqhgd,bchd->bnhgqc', qb, kc).astype(F32) * scale
    blk = jnp.arange(nb)[:, None, None] * WINDOW
    qpos = blk + jnp.arange(WINDOW)[None, :, None]
    kpos = blk - WINDOW + jnp.arange(3 * WINDOW)[None, None, :]
    valid = (jnp.abs(qpos - kpos) <= WINDOW) & (kpos >= 0) & (kpos < n)
    s_win = jnp.where(valid[None, :, None, None], s_win, NEG_INF)
    p = sink_softmax(jnp.concatenate([s_win, s_ctx], axis=-1), sinks)
    p_win, p_ctx = p[..., :3 * WINDOW], p[..., 3 * WINDOW:]
    o = (jnp.einsum('bnhgqj,bnjhd->bnqhgd', p_win.astype(v.dtype), vw)
         + jnp.einsum('bnhgqc,bchd->bnqhgd', p_ctx.astype(vc.dtype), vc))
    return o.reshape(b, n, B_Q)


def sink_attention_ctx(qc, kc, vc, sinks):
    b, n = qc.shape[:2]
    qg = qc.reshape(b, n, B_KV_HEADS, B_GROUP, HEAD_DIM)
    s = jnp.einsum('bqhgd,bchd->bhgqc', qg, kc).astype(F32) * (HEAD_DIM ** -0.5)
    p = sink_softmax(s, sinks)
    o = jnp.einsum('bhgqc,bchd->bqhgd', p.astype(vc.dtype), vc)
    return o.reshape(b, n, B_Q)


def gated_merge(oa, ob, ga, gb, w_oA, w_oB, w_out):
    y = jax.nn.sigmoid(ga) * (oa @ w_oA) + jax.nn.sigmoid(gb) * (ob @ w_oB)
    return y @ w_out


def token_mixer(h, hc, rope, lam, subln_w, sinks, w_in, w_oA, w_oB, w_out, lam_init, with_ctx):
    n, n_ctx = h.shape[1], hc.shape[1]
    qa, ka, va, qb, kb, vb, ga, gb = split_projection(h @ w_in, n)
    qac, kac, vac, qbc, kbc, vbc, gac, gbc = split_projection(hc @ w_in, n_ctx)
    qa, ka, qb, kb = (axial_rope(t, rope) for t in (qa, ka, qb, kb))
    oa = diff_attention_latent(qa, ka, va, kac, vac, lam, subln_w, lam_init)
    ob = window_attention_latent(qb, kb, vb, kbc, vbc, sinks)
    out = gated_merge(oa, ob, ga, gb, w_oA, w_oB, w_out)
    if with_ctx:
        oac = diff_attention_ctx(qac, kac, vac, lam, subln_w, lam_init)
        obc = sink_attention_ctx(qbc, kbc, vbc, sinks)
        return out, gated_merge(oac, obc, gac, gbc, w_oA, w_oB, w_out)
    return out, None


def clamped_swiglu(gu):
    x_glu, x_lin = jnp.split(gu, 2, axis=-1)
    x_glu = jnp.minimum(x_glu, SWIGLU_LIMIT)
    x_lin = jnp.clip(x_lin, -SWIGLU_LIMIT, SWIGLU_LIMIT)
    return x_glu * jax.nn.sigmoid(SWIGLU_ALPHA * x_glu) * (x_lin + 1.0)


def moe_ffn(h, w_router, b_router, w_up, b_up, w_down, b_down):
    shp = h.shape
    t = h.reshape(-1, D_MODEL)
    n_tok = t.shape[0]
    n_assign = n_tok * TOP_K
    logits = (t @ w_router + b_router).astype(F32)
    top_logit, top_e = lax.top_k(logits, TOP_K)
    gate = jax.nn.softmax(top_logit, axis=-1).astype(h.dtype)
    e_flat = top_e.reshape(-1)
    tok_flat = jnp.repeat(jnp.arange(n_tok, dtype=jnp.int32), TOP_K)
    g_flat = gate.reshape(-1)
    order = jnp.argsort(e_flat, stable=True)
    se, stok, sg = e_flat[order], tok_flat[order], g_flat[order]
    counts = jnp.bincount(e_flat, length=N_EXPERTS)
    padded = (counts + MOE_BLOCK - 1) // MOE_BLOCK * MOE_BLOCK
    start = jnp.cumsum(counts) - counts
    pend = jnp.cumsum(padded)
    pstart = pend - padded
    pos = pstart[se] + jnp.arange(n_assign) - start[se]
    n_blocks = -(-n_assign // MOE_BLOCK) + N_EXPERTS
    n_rows = n_blocks * MOE_BLOCK
    row_tok = jnp.zeros((n_rows,), jnp.int32).at[pos].set(stok)
    row_gate = jnp.zeros((n_rows,), h.dtype).at[pos].set(sg)
    blk_e = jnp.minimum(jnp.searchsorted(pend, jnp.arange(n_blocks) * MOE_BLOCK, side='right'),
                        N_EXPERTS - 1)

    def expert_block(args):
        e, toks = args
        xb = t[toks]
        gu = xb @ w_up[e] + b_up[e]
        return clamped_swiglu(gu) @ w_down[e] + b_down[e]

    y = lax.map(expert_block, (blk_e, row_tok.reshape(n_blocks, MOE_BLOCK)))
    out = jnp.zeros_like(t).at[row_tok].add(y.reshape(n_rows, D_MODEL) * row_gate[:, None])
    return out.reshape(shp)


def setup_inputs(seed: int = 0) -> dict:
    key = jax.random.key(seed)
    ks = jax.random.split(key, 26)

    def nrm(k, shape, scale):
        return jax.random.normal(k, shape, F32) * scale

    L = DEPTH
    return {
        "x": nrm(ks[0], (BATCH, SEQ, D_MODEL), 1.0),
        "c": nrm(ks[1], (BATCH, D_MODEL), 1.0),
        "ctx": nrm(ks[2], (BATCH, CTX_LEN, D_MODEL), 1.0),
        "c_ctx": nrm(ks[3], (D_MODEL,), 1.0),
        "w_ada": nrm(ks[4], (L, D_MODEL, 6 * D_MODEL), 0.5 * D_MODEL ** -0.5),
        "b_ada": nrm(ks[5], (L, 6 * D_MODEL), 0.01),
        "w_in": nrm(ks[6], (L, D_MODEL, PROJ_COLS), D_MODEL ** -0.5),
        "lam_q1": nrm(ks[7], (L, HEAD_DIM), 0.1),
        "lam_k1": nrm(ks[8], (L, HEAD_DIM), 0.1),
        "lam_q2": nrm(ks[9], (L, HEAD_DIM), 0.1),
        "lam_k2": nrm(ks[10], (L, HEAD_DIM), 0.1),
        "subln_w": 1.0 + nrm(ks[11], (L, A_V_DIM), 0.01),
        "sinks": nrm(ks[12], (L, B_HEADS), 0.5),
        "w_oA": nrm(ks[13], (L, A_V, D_MODEL), A_V ** -0.5),
        "w_oB": nrm(ks[14], (L, B_Q, D_MODEL), B_Q ** -0.5),
        "w_out": nrm(ks[15], (L, D_MODEL, D_MODEL), DEEPNORM_BETA * D_MODEL ** -0.5),
        "ln1_g": 1.0 + nrm(ks[16], (L, D_MODEL), 0.01),
        "ln1_b": nrm(ks[17], (L, D_MODEL), 0.01),
        "w_router": nrm(ks[18], (L, D_MODEL, N_EXPERTS), D_MODEL ** -0.5),
        "b_router": nrm(ks[19], (L, N_EXPERTS), 0.01),
        "w_up": nrm(ks[20], (L, N_EXPERTS, D_MODEL, 2 * D_FF), D_MODEL ** -0.5),
        "b_up": nrm(ks[21], (L, N_EXPERTS, 2 * D_FF), 0.01),
        "w_down": nrm(ks[22], (L, N_EXPERTS, D_FF, D_MODEL), DEEPNORM_BETA * D_FF ** -0.5),
        "b_down": nrm(ks[23], (L, N_EXPERTS, D_MODEL), 0.01),
        "ln2_g": 1.0 + nrm(ks[24], (L, D_MODEL), 0.01),
        "ln2_b": nrm(ks[25], (L, D_MODEL), 0.01),
    }


def reference(x, c, ctx, c_ctx, w_ada, b_ada, w_in, lam_q1, lam_k1, lam_q2, lam_k2, subln_w,
              sinks, w_oA, w_oB, w_out, ln1_g, ln1_b, w_router, b_router, w_up, b_up,
              w_down, b_down, ln2_g, ln2_b):
    rope = axial_rope_tables(x.shape[1])
    for l in range(DEPTH):
        with_ctx = l < DEPTH - 1
        lam_init = 0.8 - 0.6 * math.exp(-0.3 * l)
        lam = (jnp.exp(jnp.sum(lam_q1[l] * lam_k1[l]).astype(F32))
               - jnp.exp(jnp.sum(lam_q2[l] * lam_k2[l]).astype(F32)) + lam_init)
        sh1, sc1, g1, sh2, sc2, g2 = (m[:, None, :] for m in
                                      jnp.split(jax.nn.silu(c) @ w_ada[l] + b_ada[l], 6, axis=-1))
        sh1c, sc1c, g1c, sh2c, sc2c, g2c = jnp.split(jax.nn.silu(c_ctx) @ w_ada[l] + b_ada[l], 6, axis=-1)
        h = x * (1 + sc1) + sh1
        hc = ctx * (1 + sc1c) + sh1c
        mix, mix_c = token_mixer(h, hc, rope, lam, subln_w[l], sinks[l], w_in[l], w_oA[l],
                                 w_oB[l], w_out[l], lam_init, with_ctx)
        x = layer_norm(DEEPNORM_ALPHA * x + g1 * mix, ln1_g[l], ln1_b[l])
        ffn = (w_router[l], b_router[l], w_up[l], b_up[l], w_down[l], b_down[l])
        x = layer_norm(DEEPNORM_ALPHA * x + g2 * moe_ffn(x * (1 + sc2) + sh2, *ffn), ln2_g[l], ln2_b[l])
        if with_ctx:
            ctx = layer_norm(DEEPNORM_ALPHA * ctx + g1c * mix_c, ln1_g[l], ln1_b[l])
            ctx = layer_norm(DEEPNORM_ALPHA * ctx + g2c * moe_ffn(ctx * (1 + sc2c) + sh2c, *ffn),
                             ln2_g[l], ln2_b[l])
    return x
```

```python
import functools
import math

import jax
import jax.numpy as jnp
from jax import lax
from jax.experimental import pallas as pl
from jax.experimental.pallas import tpu as pltpu

F32 = jnp.float32
BF16 = jnp.bfloat16

D_MODEL = 2048
BATCH = 4
SEQ = 4096
DEPTH = 1
GRID_W = 64
CTX_LEN = 256
HEAD_DIM = 64
A_HEADS = 8
B_HEADS = 16
B_KV_HEADS = 2
WINDOW = 128
ROPE_THETA = 10000.0
N_EXPERTS = 32
TOP_K = 4
D_FF = D_MODEL
SWIGLU_ALPHA = 1.702
SWIGLU_LIMIT = 7.0
LN_EPS = 1e-5
NEG_INF = -1e30
DEEPNORM_ALPHA = (2 * DEPTH) ** 0.25
LAM_INIT = 0.8 - 0.6 * math.exp(-0.3 * 0)

LANES = 128
N_TOK = BATCH * SEQ

SLAB_QA, SLAB_KA, SLAB_QB, SLAB_KB, SLAB_VA, SLAB_VB = 0, 8, 16, 24, 26, 34
N_SLABS_LAT = 36
N_ROPE_SLABS = 26
CSLAB_KA, CSLAB_KB, CSLAB_VA, CSLAB_VB = 0, 8, 10, 18
N_SLABS_CTX = 20

VMEM_LIMIT = 56 * 1024 * 1024

NT_DIMS = (((1,), (1,)), ((), ()))


def _cparams(sem, vmem=VMEM_LIMIT):
    return pltpu.CompilerParams(dimension_semantics=sem, vmem_limit_bytes=vmem)


ADA_TN = 1024


def _ada_kernel(c_ref, w_ref, b_ref, o_ref):
    c = c_ref[...]
    s = (c * jax.nn.sigmoid(c)).astype(BF16)
    o_ref[...] = jnp.dot(s, w_ref[...].astype(BF16), preferred_element_type=F32) + b_ref[...]


def _ada(cc, w_ada, b_ada):
    n = w_ada.shape[1]
    return pl.pallas_call(
        _ada_kernel,
        grid=(n // ADA_TN,),
        in_specs=[pl.BlockSpec((8, D_MODEL), lambda j: (0, 0)),
                  pl.BlockSpec((D_MODEL, ADA_TN), lambda j: (0, j)),
                  pl.BlockSpec((1, ADA_TN), lambda j: (0, j))],
        out_specs=pl.BlockSpec((8, ADA_TN), lambda j: (0, j)),
        out_shape=jax.ShapeDtypeStruct((8, n), F32),
        compiler_params=_cparams(("arbitrary",)),
        name="ada",
    )(cc, w_ada, b_ada)


PROJ_SLABS_PER_TILE = 2
PROJ_TN = PROJ_SLABS_PER_TILE * LANES


def _proj_kernel(*refs, n_rope_tiles):
    if n_rope_tiles:
        x_ref, mod_ref, w_ref, cos_ref, sa_ref, sb_ref, o_ref, h_scr = refs
    else:
        x_ref, mod_ref, w_ref, o_ref, h_scr = refs
    j = pl.program_id(2)

    @pl.when(j == 0)
    def _():
        m = mod_ref[...]
        h_scr[...] = (x_ref[...] * (1.0 + m[0:1, :]) + m[1:2, :]).astype(BF16)

    acc = jnp.dot(h_scr[...], w_ref[...], preferred_element_type=F32)

    def plain():
        for s in range(PROJ_SLABS_PER_TILE):
            o_ref[s] = acc[:, s * LANES:(s + 1) * LANES].astype(BF16)

    if n_rope_tiles:
        @pl.when(j < n_rope_tiles)
        def _():
            cos, sa, sb = cos_ref[...], sa_ref[...], sb_ref[...]
            for s in range(PROJ_SLABS_PER_TILE):
                a = acc[:, s * LANES:(s + 1) * LANES]
                r = a * cos + pltpu.roll(a, LANES - 16, 1) * sa + pltpu.roll(a, 16, 1) * sb
                o_ref[s] = r.astype(BF16)

        pl.when(j >= n_rope_tiles)(plain)
    else:
        plain()


def _proj(x, mod, w, tables, tm):
    b, lx, _ = x.shape
    n_slabs = w.shape[1] // LANES
    n_col_tiles = n_slabs // PROJ_SLABS_PER_TILE
    n_rope_tiles = (N_ROPE_SLABS // PROJ_SLABS_PER_TILE) if tables is not None else 0
    mod_b = mod.shape[0]
    in_specs = [
        pl.BlockSpec((None, tm, D_MODEL), lambda bi, i, j: (bi, i, 0)),
        pl.BlockSpec((None, 2, D_MODEL), lambda bi, i, j: (bi if mod_b > 1 else 0, 0, 0)),
        pl.BlockSpec((D_MODEL, PROJ_TN), lambda bi, i, j: (0, j)),
    ]
    args = [x, mod, w]
    if tables is not None:
        in_specs += [pl.BlockSpec((tm, LANES), lambda bi, i, j: (i, 0))] * 3
        args += list(tables)
    return pl.pallas_call(
        functools.partial(_proj_kernel, n_rope_tiles=n_rope_tiles),
        grid=(b, lx // tm, n_col_tiles),
        in_specs=in_specs,
        out_specs=pl.BlockSpec((None, PROJ_SLABS_PER_TILE, tm, LANES), lambda bi, i, j: (bi, j, i, 0)),
        out_shape=jax.ShapeDtypeStruct((b, n_slabs, lx, LANES), BF16),
        scratch_shapes=[pltpu.VMEM((tm, D_MODEL), BF16)],
        compiler_params=_cparams(("parallel", "parallel", "arbitrary")),
        name="proj_rope" if tables is not None else "proj_ctx",
    )(*args)


A_TQ = 512
A_KC = 512
A_NCH = SEQ // A_KC


def _attn_a_kernel(lam_ref, q_ref, k_ref, v_ref, kc_ref, vc_ref, subw_ref, o_ref, s_scr, sc_scr):
    lp = lam_ref[...]
    lam = (jnp.exp(jnp.sum(lp[0:1] * lp[1:2], axis=-1, keepdims=True))
           - jnp.exp(jnp.sum(lp[2:3] * lp[3:4], axis=-1, keepdims=True)) + LAM_INIT)
    q = q_ref[...]
    lane = lax.broadcasted_iota(jnp.int32, (1, LANES), 1)
    zero = jnp.zeros_like(q)
    q0 = jnp.where(lane < HEAD_DIM, q, zero)
    q1 = jnp.where(lane >= HEAD_DIM, q, zero)

    kc = kc_ref[...]
    sc0 = lax.dot_general(q0, kc, NT_DIMS, preferred_element_type=F32)
    sc1 = lax.dot_general(q1, kc, NT_DIMS, preferred_element_type=F32)
    m0 = jnp.max(sc0, axis=-1, keepdims=True)
    m1 = jnp.max(sc1, axis=-1, keepdims=True)

    def scores(c, carry):
        m0, m1 = carry
        kk = k_ref[pl.ds(pl.multiple_of(c * A_KC, A_KC), A_KC), :]
        s0 = lax.dot_general(q0, kk, NT_DIMS, preferred_element_type=F32)
        s1 = lax.dot_general(q1, kk, NT_DIMS, preferred_element_type=F32)
        s_scr[0, c] = s0
        s_scr[1, c] = s1
        return (jnp.maximum(m0, jnp.max(s0, axis=-1, keepdims=True)),
                jnp.maximum(m1, jnp.max(s1, axis=-1, keepdims=True)))

    m0, m1 = lax.fori_loop(0, A_NCH, scores, (m0, m1))

    pc0 = jnp.exp(sc0 - m0)
    pc1 = jnp.exp(sc1 - m1)
    sc_scr[0] = pc0
    sc_scr[1] = pc1
    l0 = jnp.sum(pc0, axis=-1, keepdims=True)
    l1 = jnp.sum(pc1, axis=-1, keepdims=True)

    def probs(c, carry):
        l0, l1 = carry
        p0 = jnp.exp(s_scr[0, c] - m0)
        p1 = jnp.exp(s_scr[1, c] - m1)
        s_scr[0, c] = p0
        s_scr[1, c] = p1
        return (l0 + jnp.sum(p0, axis=-1, keepdims=True), l1 + jnp.sum(p1, axis=-1, keepdims=True))

    l0, l1 = lax.fori_loop(0, A_NCH, probs, (l0, l1))
    w0 = 1.0 / l0
    w1 = lam / l1

    a_ctx = (sc_scr[0] * w0 - sc_scr[1] * w1).astype(BF16)
    acc = jnp.dot(a_ctx, vc_ref[...], preferred_element_type=F32)

    def pv(c, acc):
        a = (s_scr[0, c] * w0 - s_scr[1, c] * w1).astype(BF16)
        vv = v_ref[pl.ds(pl.multiple_of(c * A_KC, A_KC), A_KC), :]
        return acc + jnp.dot(a, vv, preferred_element_type=F32)

    o = lax.fori_loop(0, A_NCH, pv, acc)
    ms = jnp.mean(o * o, axis=-1, keepdims=True)
    o_ref[...] = (o * lax.rsqrt(ms + LN_EPS) * subw_ref[...] * (1.0 - LAM_INIT)).astype(BF16)


def _attn_a(lam_p, p_lat, p_ctx, subw):
    return pl.pallas_call(
        _attn_a_kernel,
        grid=(BATCH, A_HEADS, SEQ // A_TQ),
        in_specs=[
            pl.BlockSpec((4, HEAD_DIM), lambda b, h, i: (0, 0)),
            pl.BlockSpec((None, None, A_TQ, LANES), lambda b, h, i: (b, SLAB_QA + h, i, 0)),
            pl.BlockSpec((None, None, SEQ, LANES), lambda b, h, i: (b, SLAB_KA + h, 0, 0)),
            pl.BlockSpec((None, None, SEQ, LANES), lambda b, h, i: (b, SLAB_VA + h, 0, 0)),
            pl.BlockSpec((None, None, CTX_LEN, LANES), lambda b, h, i: (b, CSLAB_KA + h, 0, 0)),
            pl.BlockSpec((None, None, CTX_LEN, LANES), lambda b, h, i: (b, CSLAB_VA + h, 0, 0)),
            pl.BlockSpec((1, LANES), lambda b, h, i: (0, 0)),
        ],
        out_specs=pl.BlockSpec((None, A_TQ, LANES), lambda b, h, i: (b, i, h)),
        out_shape=jax.ShapeDtypeStruct((BATCH, SEQ, A_HEADS * LANES), BF16),
        scratch_shapes=[pltpu.VMEM((2, A_NCH, A_TQ, A_KC), F32),
                        pltpu.VMEM((2, A_TQ, CTX_LEN), F32)],
        compiler_params=_cparams(("parallel", "parallel", "arbitrary")),
        name="attn_a",
    )(lam_p, p_lat, p_lat, p_lat, p_ctx, p_ctx, subw)


B_TQ = 256
B_KW = B_TQ + 2 * WINDOW
B_PAIRS = B_HEADS // 2
B_PAIRS_PER_KV = B_PAIRS // B_KV_HEADS


def _attn_b_kernel(sinks_ref, q_ref, k_ref, v_ref, kc_ref, vc_ref, o_ref):
    i = pl.program_id(1)
    kstart = pl.multiple_of(jnp.clip(i * B_TQ - WINDOW, 0, SEQ - B_KW), WINDOW)
    qpos = i * B_TQ + lax.broadcasted_iota(jnp.int32, (B_TQ, B_KW), 0)
    kpos = kstart + lax.broadcasted_iota(jnp.int32, (B_TQ, B_KW), 1)
    valid = jnp.abs(qpos - kpos) <= WINDOW
    lane = lax.broadcasted_iota(jnp.int32, (1, LANES), 1)
    lo = lane < HEAD_DIM

    def pair(p, _):
        hk = p // B_PAIRS_PER_KV
        q = q_ref[p]
        kw = k_ref[hk, pl.ds(kstart, B_KW), :]
        vw = v_ref[hk, pl.ds(kstart, B_KW), :]
        kcx = kc_ref[hk]
        vcx = vc_ref[hk]
        zero = jnp.zeros_like(q)
        outs = []
        for par in range(2):
            qm = jnp.where(lo if par == 0 else jnp.logical_not(lo), q, zero)
            sw = lax.dot_general(qm, kw, NT_DIMS, preferred_element_type=F32)
            sw = jnp.where(valid, sw, NEG_INF)
            sx = lax.dot_general(qm, kcx, NT_DIMS, preferred_element_type=F32)
            sink = sinks_ref[2 * p + par]
            m = jnp.maximum(jnp.maximum(jnp.max(sw, axis=-1, keepdims=True),
                                        jnp.max(sx, axis=-1, keepdims=True)), sink)
            ew = jnp.exp(sw - m)
            ex = jnp.exp(sx - m)
            den = (jnp.sum(ew, axis=-1, keepdims=True) + jnp.sum(ex, axis=-1, keepdims=True)
                   + jnp.exp(sink - m))
            inv = 1.0 / den
            outs.append(jnp.dot((ew * inv).astype(BF16), vw, preferred_element_type=F32)
                        + jnp.dot((ex * inv).astype(BF16), vcx, preferred_element_type=F32))
        o_ref[p] = jnp.where(lo, outs[0], outs[1]).astype(BF16)
        return 0

    lax.fori_loop(0, B_PAIRS, pair, 0)


def _attn_b(sinks, p_lat, p_ctx):
    return pl.pallas_call(
        _attn_b_kernel,
        grid=(BATCH, SEQ // B_TQ),
        in_specs=[
            pl.BlockSpec(memory_space=pltpu.SMEM),
            pl.BlockSpec((None, B_PAIRS, B_TQ, LANES), lambda b, i: (b, SLAB_QB // B_PAIRS, i, 0)),
            pl.BlockSpec((None, B_KV_HEADS, SEQ, LANES), lambda b, i: (b, SLAB_KB // B_KV_HEADS, 0, 0)),
            pl.BlockSpec((None, B_KV_HEADS, SEQ, LANES), lambda b, i: (b, SLAB_VB // B_KV_HEADS, 0, 0)),
            pl.BlockSpec((None, B_KV_HEADS, CTX_LEN, LANES), lambda b, i: (b, CSLAB_KB // B_KV_HEADS, 0, 0)),
            pl.BlockSpec((None, B_KV_HEADS, CTX_LEN, LANES), lambda b, i: (b, CSLAB_VB // B_KV_HEADS, 0, 0)),
        ],
        out_specs=pl.BlockSpec((None, B_PAIRS, B_TQ, LANES), lambda b, i: (b, 0, i, 0)),
        out_shape=jax.ShapeDtypeStruct((BATCH, B_PAIRS, SEQ, LANES), BF16),
        compiler_params=_cparams(("parallel", "arbitrary")),
        name="attn_b",
    )(sinks, p_lat, p_lat, p_lat, p_ctx, p_ctx)


M_TM = 512
M_TN = 256
M_NJ = D_MODEL // M_TN
M_TILES_PER_B = SEQ // M_TM
M_FIN_ROWS = 128
MOD_SC1, MOD_SH1, MOD_G1, MOD_SC2, MOD_SH2, MOD_G2 = 0, 1, 2, 3, 4, 5


def _layer_norm(v, g, b):
    mu = jnp.mean(v, axis=-1, keepdims=True)
    d = v - mu
    var = jnp.mean(d * d, axis=-1, keepdims=True)
    return d * lax.rsqrt(var + LN_EPS) * g + b


def _merge_kernel(x_ref, mod_ref, oa_ref, ob_ref, wga_ref, wgb_ref, woa_ref, wob_ref, wout_ref,
                  lng_ref, lnb_ref, wr_ref, br_ref,
                  x1_ref, h2_ref, te_ref, tg_ref, h_scr, ob_scr):
    j = pl.program_id(1)

    @pl.when(j == 0)
    def _():
        m = mod_ref[...]
        h_scr[...] = (x_ref[...] * (1.0 + m[MOD_SC1:MOD_SC1 + 1, :]) + m[MOD_SH1:MOD_SH1 + 1, :]).astype(BF16)
        for p in range(B_PAIRS):
            ob_scr[:, p * LANES:(p + 1) * LANES] = ob_ref[p]
        x1_ref[...] = jnp.zeros_like(x1_ref)

    h = h_scr[...]
    ga = jnp.dot(h, wga_ref[...], preferred_element_type=F32)
    gb = jnp.dot(h, wgb_ref[...], preferred_element_type=F32)
    ta = jnp.dot(oa_ref[...], woa_ref[...], preferred_element_type=F32)
    tb = jnp.dot(ob_scr[...], wob_ref[...], preferred_element_type=F32)
    y = (jax.nn.sigmoid(ga) * ta + jax.nn.sigmoid(gb) * tb).astype(BF16)
    x1_ref[...] += jnp.dot(y, wout_ref[...], preferred_element_type=F32)

    @pl.when(j == M_NJ - 1)
    def _():
        m = mod_ref[...]
        lane32 = lax.broadcasted_iota(jnp.int32, (M_FIN_ROWS, N_EXPERTS), 1)
        lane128 = lax.broadcasted_iota(jnp.int32, (M_FIN_ROWS, LANES), 1)
        for r in range(M_TM // M_FIN_ROWS):
            rows = pl.ds(r * M_FIN_ROWS, M_FIN_ROWS)
            pre = DEEPNORM_ALPHA * x_ref[rows, :] + m[MOD_G1:MOD_G1 + 1, :] * x1_ref[rows, :]
            x1 = _layer_norm(pre, lng_ref[...], lnb_ref[...])
            x1_ref[rows, :] = x1
            h2 = x1 * (1.0 + m[MOD_SC2:MOD_SC2 + 1, :]) + m[MOD_SH2:MOD_SH2 + 1, :]
            h2_ref[rows, :] = h2
            logits = jnp.dot(h2, wr_ref[...], preferred_element_type=F32,
                             precision=lax.Precision.HIGHEST) + br_ref[...]
            tops, idxs = [], []
            for _ in range(TOP_K):
                mk = jnp.max(logits, axis=-1, keepdims=True)
                ik = jnp.min(jnp.where(logits == mk, lane32, N_EXPERTS), axis=-1, keepdims=True)
                tops.append(mk)
                idxs.append(ik)
                logits = jnp.where(lane32 == ik, -jnp.inf, logits)
            es = [jnp.exp(t - tops[0]) for t in tops]
            den = es[0] + es[1] + es[2] + es[3]
            te = jnp.zeros((M_FIN_ROWS, LANES), jnp.int32)
            tg = jnp.zeros((M_FIN_ROWS, LANES), F32)
            for k in range(TOP_K):
                te = jnp.where(lane128 == k, idxs[k], te)
                tg = jnp.where(lane128 == k, es[k] / den, tg)
            te_ref[rows, :] = te
            tg_ref[rows, :] = tg


def _merge(x2d, modl, oa2d, ob, w_g, w_oa, w_ob, w_out, ln_g, ln_b, w_r, b_r):
    row = lambda i, j: (i, 0)
    return pl.pallas_call(
        _merge_kernel,
        grid=(N_TOK // M_TM, M_NJ),
        in_specs=[
            pl.BlockSpec((M_TM, D_MODEL), row),
            pl.BlockSpec((None, 8, D_MODEL), lambda i, j: (i // M_TILES_PER_B, 0, 0)),
            pl.BlockSpec((M_TM, A_HEADS * LANES), row),
            pl.BlockSpec((None, B_PAIRS, M_TM, LANES), lambda i, j: (i // M_TILES_PER_B, 0, i % M_TILES_PER_B, 0)),
            pl.BlockSpec((D_MODEL, M_TN), lambda i, j: (0, j)),
            pl.BlockSpec((D_MODEL, M_TN), lambda i, j: (0, M_NJ + j)),
            pl.BlockSpec((A_HEADS * LANES, M_TN), lambda i, j: (0, j)),
            pl.BlockSpec((B_PAIRS * LANES, M_TN), lambda i, j: (0, j)),
            pl.BlockSpec((M_TN, D_MODEL), lambda i, j: (j, 0)),
            pl.BlockSpec((1, D_MODEL), lambda i, j: (0, 0)),
            pl.BlockSpec((1, D_MODEL), lambda i, j: (0, 0)),
            pl.BlockSpec((D_MODEL, N_EXPERTS), lambda i, j: (0, 0)),
            pl.BlockSpec((1, N_EXPERTS), lambda i, j: (0, 0)),
        ],
        out_specs=[pl.BlockSpec((M_TM, D_MODEL), row), pl.BlockSpec((M_TM, D_MODEL), row),
                   pl.BlockSpec((M_TM, LANES), row), pl.BlockSpec((M_TM, LANES), row)],
        out_shape=[jax.ShapeDtypeStruct((N_TOK, D_MODEL), F32), jax.ShapeDtypeStruct((N_TOK, D_MODEL), F32),
                   jax.ShapeDtypeStruct((N_TOK, LANES), jnp.int32), jax.ShapeDtypeStruct((N_TOK, LANES), F32)],
        scratch_shapes=[pltpu.VMEM((M_TM, D_MODEL), BF16), pltpu.VMEM((M_TM, B_PAIRS * LANES), BF16)],
        compiler_params=_cparams(("parallel", "arbitrary")),
        name="merge_ln1_router",
    )(x2d, modl, oa2d, ob, w_g, w_g, w_oa, w_ob, w_out, ln_g, ln_b, w_r, b_r)


E_TM = 1024
E_SUB = 256
E_TF = 256
E_NJ = D_FF // E_TF
E_NBLK = (N_TOK * TOP_K) // E_TM + N_EXPERTS
E_NROWS = E_NBLK * E_TM


def _dispatch_kernel(used_ref, rowtok_ref, h2_hbm, o_ref, buf, sem):
    i = pl.program_id(0)

    def row_copy(r, tok):
        return pltpu.make_async_copy(h2_hbm.at[pl.ds(tok, 1)], buf.at[pl.ds(r, 1)], sem)

    @pl.when(i < used_ref[0])
    def _():
        def issue(r, _):
            row_copy(r, rowtok_ref[0, 0, r]).start()
            return 0

        lax.fori_loop(0, E_TM, issue, 0)

        def wait(r, _):
            row_copy(r, 0).wait()
            return 0

        lax.fori_loop(0, E_TM, wait, 0)
        o_ref[...] = buf[...].astype(BF16)

    @pl.when(i >= used_ref[0])
    def _():
        o_ref[...] = jnp.zeros_like(o_ref)


def _dispatch(used, row_tok3, h2):
    return pl.pallas_call(
        _dispatch_kernel,
        grid=(E_NBLK,),
        in_specs=[pl.BlockSpec(memory_space=pltpu.SMEM),
                  pl.BlockSpec((1, 1, E_TM), lambda i: (i, 0, 0), memory_space=pltpu.SMEM),
                  pl.BlockSpec(memory_space=pl.ANY)],
        out_specs=pl.BlockSpec((E_TM, D_MODEL), lambda i: (i, 0)),
        out_shape=jax.ShapeDtypeStruct((E_NROWS, D_MODEL), BF16),
        scratch_shapes=[pltpu.VMEM((E_TM, D_MODEL), F32), pltpu.SemaphoreType.DMA(())],
        compiler_params=_cparams(("arbitrary",)),
        name="moe_dispatch",
    )(used, row_tok3, h2)


def _expert_kernel(blk_e_ref, blk_nsub_ref, blk_src_ref, used_ref,
                   x_ref, wg_ref, wl_ref, bg_ref, bl_ref, wd_ref, bd_ref, o_ref,
                   wg_s, wl_s, wd_s):
    i = pl.program_id(0)
    j = pl.program_id(1)

    @pl.when(i < used_ref[0])
    def _():
        @pl.when(j == 0)
        def _():
            o_ref[...] = jnp.broadcast_to(bd_ref[...], o_ref.shape)

        wg_s[...] = wg_ref[...].astype(BF16)
        wl_s[...] = wl_ref[...].astype(BF16)
        wd_s[...] = wd_ref[...].astype(BF16)

        def sub(s, _):
            rows = pl.ds(pl.multiple_of(s * E_SUB, E_SUB), E_SUB)
            xs = x_ref[rows, :]
            g = jnp.dot(xs, wg_s[...], preferred_element_type=F32) + bg_ref[...]
            l = jnp.dot(xs, wl_s[...], preferred_element_type=F32) + bl_ref[...]
            g = jnp.minimum(g, SWIGLU_LIMIT)
            l = jnp.clip(l, -SWIGLU_LIMIT, SWIGLU_LIMIT)
            act = (g * jax.nn.sigmoid(SWIGLU_ALPHA * g) * (l + 1.0)).astype(BF16)
            o_ref[rows, :] += jnp.dot(act, wd_s[...], preferred_element_type=F32)
            return 0

        lax.fori_loop(0, blk_nsub_ref[i], sub, 0)

    @pl.when(jnp.logical_and(i >= used_ref[0], j == 0))
    def _():
        o_ref[...] = jnp.zeros_like(o_ref)


def _experts(blk_e, blk_nsub, blk_src, used, x_sorted, w_up, b_up, w_down, b_down):
    def jj(i, j, used):
        return jnp.where(i < used[0], j, E_NJ - 1)

    grid_spec = pltpu.PrefetchScalarGridSpec(
        num_scalar_prefetch=4,
        grid=(E_NBLK, E_NJ),
        in_specs=[
            pl.BlockSpec((E_TM, D_MODEL), lambda i, j, be, bn, bs, u: (bs[i], 0)),
            pl.BlockSpec((None, D_MODEL, E_TF), lambda i, j, be, bn, bs, u: (be[i], 0, jj(i, j, u))),
            pl.BlockSpec((None, D_MODEL, E_TF), lambda i, j, be, bn, bs, u: (be[i], 0, E_NJ + jj(i, j, u))),
            pl.BlockSpec((None, 1, E_TF), lambda i, j, be, bn, bs, u: (be[i], 0, jj(i, j, u))),
            pl.BlockSpec((None, 1, E_TF), lambda i, j, be, bn, bs, u: (be[i], 0, E_NJ + jj(i, j, u))),
            pl.BlockSpec((None, E_TF, D_MODEL), lambda i, j, be, bn, bs, u: (be[i], jj(i, j, u), 0)),
            pl.BlockSpec((None, 1, D_MODEL), lambda i, j, be, bn, bs, u: (be[i], 0, 0)),
        ],
        out_specs=pl.BlockSpec((E_TM, D_MODEL), lambda i, j, be, bn, bs, u: (i, 0)),
        scratch_shapes=[pltpu.VMEM((D_MODEL, E_TF), BF16), pltpu.VMEM((D_MODEL, E_TF), BF16),
                        pltpu.VMEM((E_TF, D_MODEL), BF16)],
    )
    return pl.pallas_call(
        _expert_kernel,
        grid_spec=grid_spec,
        out_shape=jax.ShapeDtypeStruct((E_NROWS, D_MODEL), F32),
        compiler_params=_cparams(("arbitrary", "arbitrary")),
        name="moe_experts",
    )(blk_e, blk_nsub, blk_src, used, x_sorted, w_up, w_up, b_up, b_up, w_down, b_down)


C_TM = 256


def _combine_kernel(pos_ref, x1_ref, mod_ref, tg_ref, lng_ref, lnb_ref, y_hbm, o_ref, ybuf, sem):
    def row_copy(t, k, row):
        return pltpu.make_async_copy(y_hbm.at[pl.ds(row, 1)], ybuf.at[k, pl.ds(t, 1)], sem)

    def issue(t, _):
        for k in range(TOP_K):
            row_copy(t, k, pos_ref[0, 0, t * TOP_K + k]).start()
        return 0

    lax.fori_loop(0, C_TM, issue, 0)

    def wait(t, _):
        for k in range(TOP_K):
            row_copy(t, k, 0).wait()
        return 0

    lax.fori_loop(0, C_TM, wait, 0)

    tg = tg_ref[...]
    ffn = tg[:, 0:1] * ybuf[0]
    for k in range(1, TOP_K):
        ffn = ffn + tg[:, k:k + 1] * ybuf[k]
    m = mod_ref[...]
    pre = DEEPNORM_ALPHA * x1_ref[...] + m[MOD_G2:MOD_G2 + 1, :] * ffn
    o_ref[...] = _layer_norm(pre, lng_ref[...], lnb_ref[...])


def _combine(pos3, x1, modl, tg, ln_g, ln_b, y):
    tiles_per_b = SEQ // C_TM
    return pl.pallas_call(
        _combine_kernel,
        grid=(N_TOK // C_TM,),
        in_specs=[
            pl.BlockSpec((1, 1, C_TM * TOP_K), lambda i: (i, 0, 0), memory_space=pltpu.SMEM),
            pl.BlockSpec((C_TM, D_MODEL), lambda i: (i, 0)),
            pl.BlockSpec((None, 8, D_MODEL), lambda i: (i // tiles_per_b, 0, 0)),
            pl.BlockSpec((C_TM, LANES), lambda i: (i, 0)),
            pl.BlockSpec((1, D_MODEL), lambda i: (0, 0)),
            pl.BlockSpec((1, D_MODEL), lambda i: (0, 0)),
            pl.BlockSpec(memory_space=pl.ANY),
        ],
        out_specs=pl.BlockSpec((C_TM, D_MODEL), lambda i: (i, 0)),
        out_shape=jax.ShapeDtypeStruct((N_TOK, D_MODEL), F32),
        scratch_shapes=[pltpu.VMEM((TOP_K, C_TM, D_MODEL), F32), pltpu.SemaphoreType.DMA(())],
        compiler_params=_cparams(("arbitrary",)),
        name="moe_combine_ln2",
    )(pos3, x1, modl, tg, ln_g, ln_b, y)


def _rope_tables():
    t = jnp.arange(SEQ)
    row = (t // GRID_W).astype(F32)
    col = (t % GRID_W).astype(F32)
    half = HEAD_DIM // 2
    inv = ROPE_THETA ** (-jnp.arange(0, half, 2, dtype=F32) / half)
    d = jnp.arange(LANES) % HEAD_DIM
    freq = inv[d % (half // 2)]
    is_col = d >= half
    second = (d % half) >= (half // 2)
    ang = jnp.where(is_col[None, :], col[:, None] * freq[None, :], row[:, None] * freq[None, :])
    cos, sin = jnp.cos(ang), jnp.sin(ang)
    sin_a = jnp.where(second[None, :], 0.0, -sin)
    sin_b = jnp.where(second[None, :], sin, 0.0)
    return cos, sin_a, sin_b


def _routing_tables(top_e):
    e_flat = top_e.reshape(-1)
    n_assign = e_flat.shape[0]
    onehot = (e_flat[:, None] == jnp.arange(N_EXPERTS, dtype=jnp.int32)[None, :]).astype(jnp.int32)
    csum = jnp.cumsum(onehot, axis=0)
    rank = jnp.sum(csum * onehot, axis=1) - 1
    counts = csum[-1]
    nblk = (counts + E_TM - 1) // E_TM
    bend = jnp.cumsum(nblk)
    bstart = bend - nblk
    used = bend[-1]
    pos = bstart[e_flat] * E_TM + rank
    row_tok = jnp.zeros((E_NROWS,), jnp.int32).at[pos].set(
        jnp.arange(n_assign, dtype=jnp.int32) // TOP_K)
    blk = jnp.arange(E_NBLK, dtype=jnp.int32)
    blk_src = jnp.minimum(blk, used - 1)
    blk_e = jnp.minimum(jnp.sum((bend[None, :] <= blk_src[:, None]).astype(jnp.int32), axis=1),
                        N_EXPERTS - 1)
    valid = jnp.clip(counts[blk_e] - (blk_src - bstart[blk_e]) * E_TM, 0, E_TM)
    blk_nsub = jnp.where(blk < used, (valid + E_SUB - 1) // E_SUB, 0).astype(jnp.int32)
    return (pos.astype(jnp.int32), row_tok, blk_e, blk_nsub, blk_src.astype(jnp.int32),
            used.reshape(1).astype(jnp.int32))


def kernel(x, c, ctx, c_ctx, w_ada, b_ada, w_in, lam_q1, lam_k1, lam_q2, lam_k2, subln_w, sinks,
           w_oA, w_oB, w_out, ln1_g, ln1_b, w_router, b_router, w_up, b_up, w_down, b_down,
           ln2_g, ln2_b):
    cc = jnp.zeros((8, D_MODEL), F32).at[:BATCH].set(c).at[BATCH].set(c_ctx)
    ada = _ada(cc, w_ada[0], b_ada[0][None, :])
    sh1, sc1, g1, sh2, sc2, g2 = jnp.split(ada, 6, axis=-1)
    zeros = jnp.zeros_like(sc1[:BATCH])
    modl = jnp.stack([sc1[:BATCH], sh1[:BATCH], g1[:BATCH], sc2[:BATCH], sh2[:BATCH], g2[:BATCH],
                      zeros, zeros], axis=1)
    mod_ctx = jnp.stack([sc1[BATCH], sh1[BATCH]], axis=0)[None]

    w = w_in[0]
    scale = HEAD_DIM ** -0.5
    qa, ka, va = w[:, 0:1024] * scale, w[:, 1024:2048], w[:, 2048:3072]
    qb, kb, vb = w[:, 3072:4096] * scale, w[:, 4096:4224], w[:, 4224:4352]
    dup = lambda t: jnp.concatenate([t[:, :HEAD_DIM], t[:, :HEAD_DIM], t[:, HEAD_DIM:], t[:, HEAD_DIM:]], axis=1)
    w_lat = jnp.concatenate([qa, ka, qb, dup(kb), va, dup(vb)], axis=1).astype(BF16)
    w_ctx = jnp.concatenate([ka, dup(kb), va, dup(vb)], axis=1).astype(BF16)
    w_g = w[:, 4352:].astype(BF16)

    p_lat = _proj(x, modl[:, :2], w_lat, _rope_tables(), tm=1024)
    p_ctx = _proj(ctx, mod_ctx, w_ctx, None, tm=CTX_LEN)
    lam_p = jnp.stack([lam_q1[0], lam_k1[0], lam_q2[0], lam_k2[0]], axis=0)
    oa = _attn_a(lam_p, p_lat, p_ctx, subln_w[0][None, :])
    ob = _attn_b(sinks[0], p_lat, p_ctx)

    x2d = x.reshape(N_TOK, D_MODEL)
    x1, h2, te, tg = _merge(x2d, modl, oa.reshape(N_TOK, A_HEADS * LANES), ob, w_g,
                            w_oA[0].astype(BF16), w_oB[0].astype(BF16), w_out[0].astype(BF16),
                            ln1_g[0][None, :], ln1_b[0][None, :], w_router[0], b_router[0][None, :])

    pos, row_tok, blk_e, blk_nsub, blk_src, used = _routing_tables(te[:, :TOP_K])
    x_sorted = _dispatch(used, row_tok.reshape(E_NBLK, 1, E_TM), h2)
    y = _experts(blk_e, blk_nsub, blk_src, used, x_sorted, w_up[0], b_up[0][:, None, :],
                 w_down[0], b_down[0][:, None, :])
    out = _combine(pos.reshape(N_TOK // C_TM, 1, C_TM * TOP_K), x1, modl, tg,
                   ln2_g[0][None, :], ln2_b[0][None, :], y)
    return out.reshape(BATCH, SEQ, D_MODEL)
```
